```python
import math
import jax, jax.numpy as jnp
from jax import lax
import numpy as np

D_MODEL = 1024
BATCH = 8
SEQ = 4096
DEPTH = 2

CHUNK = 64
N_MIXERS = 2
D_FF = 2816
S5_WIDTH = D_MODEL
S5_GROUP_CH = 16
S5_GROUPS = S5_WIDTH // S5_GROUP_CH
S5_STATE = 64
DT_MIN = 0.001
DT_MAX = 0.1
GMLP_BLOCK = 128
GMLP_WIDTH = D_MODEL
GMLP_HEADS = 8
GMLP_HEAD_CH = GMLP_WIDTH // GMLP_HEADS
DN_ALPHA = (2.0 * DEPTH) ** 0.25
DN_BETA = (8.0 * DEPTH) ** -0.25
LN_EPS = 1e-5
N_S5_LAYERS = (DEPTH + 1) // 2
N_GMLP_LAYERS = DEPTH // 2

kernel_name = "hybrid_s5_gmlp_macaron_deepnorm"


def _layernorm(x, g, b):
    xf = x.astype(jnp.float32)
    mu = jnp.mean(xf, axis=-1, keepdims=True)
    var = jnp.mean(jnp.square(xf - mu), axis=-1, keepdims=True)
    y = (xf - mu) * lax.rsqrt(var + LN_EPS)
    return (y * g.astype(jnp.float32) + b.astype(jnp.float32)).astype(x.dtype)


def _swiglu(x, w_in, w_out):
    gate, up = jnp.split(x @ w_in, 2, axis=-1)
    return (jax.nn.silu(gate) * up) @ w_out


def _complex_linear_combine(earlier, later):
    a1r, a1i, b1r, b1i = earlier
    a2r, a2i, b2r, b2i = later
    ar = a2r * a1r - a2i * a1i
    ai = a2r * a1i + a2i * a1r
    br = a2r * b1r - a2i * b1i + b2r
    bi = a2r * b1i + a2i * b1r + b2i
    return (ar, ai, br, bi)


def _s5_mixer(x, w_in, a_re, a_im, log_dt, b_re, b_im, c_re, c_im, d, w_glu, b_glu, w_out):
    f32 = jnp.float32
    bsz, seq, _ = x.shape
    u = (x @ w_in).astype(f32)
    ug = u.reshape(bsz, seq, S5_GROUPS, S5_GROUP_CH)
    ar = a_re.astype(f32)
    ai = a_im.astype(f32)
    dt = jnp.exp(log_dt.astype(f32))[:, None]
    mag = jnp.exp(ar * dt)
    lam_r = mag * jnp.cos(ai * dt)
    lam_i = mag * jnp.sin(ai * dt)
    den = ar * ar + ai * ai
    coef_r = ((lam_r - 1.0) * ar + lam_i * ai) / den
    coef_i = (lam_i * ar - (lam_r - 1.0) * ai) / den
    br = b_re.astype(f32)
    bi = b_im.astype(f32)
    bbar_r = coef_r[..., None] * br - coef_i[..., None] * bi
    bbar_i = coef_r[..., None] * bi + coef_i[..., None] * br
    bu_r = jnp.einsum('blgc,gpc->lbgp', ug, bbar_r)
    bu_i = jnp.einsum('blgc,gpc->lbgp', ug, bbar_i)
    lam_r_seq = jnp.broadcast_to(lam_r[None, None], (seq, 1, S5_GROUPS, S5_STATE))
    lam_i_seq = jnp.broadcast_to(lam_i[None, None], (seq, 1, S5_GROUPS, S5_STATE))
    _, _, h_r, h_i = lax.associative_scan(
        _complex_linear_combine, (lam_r_seq, lam_i_seq, bu_r, bu_i), axis=0)
    y = (jnp.einsum('lbgp,gcp->blgc', h_r, c_re.astype(f32))
         - jnp.einsum('lbgp,gcp->blgc', h_i, c_im.astype(f32)))
    y = y.reshape(bsz, seq, S5_WIDTH) + d.astype(f32) * u
    y = jax.nn.gelu(y)
    z = y * jax.nn.sigmoid(y @ w_glu.astype(f32) + b_glu.astype(f32))
    return z.astype(x.dtype) @ w_out


def _chunk_causal_block_mask():
    pos = jnp.arange(GMLP_BLOCK)
    return (pos[None, :] // CHUNK) <= (pos[:, None] // CHUNK)


def _gmlp_mixer(x, w_in, b_in, ln_g, ln_b, w_s, b_s, w_out):
    bsz, seq, _ = x.shape
    z = jax.nn.gelu(x @ w_in + b_in)
    u, v = jnp.split(z, 2, axis=-1)
    v = _layernorm(v, ln_g, ln_b)
    v = v.reshape(bsz, seq // GMLP_BLOCK, GMLP_BLOCK, GMLP_HEADS, GMLP_HEAD_CH)
    w = jnp.where(_chunk_causal_block_mask()[None], w_s, jnp.zeros((), w_s.dtype))
    s = jnp.einsum('hij,bnjhc->bnihc', w, v) + jnp.transpose(b_s)[:, :, None]
    gated = u * s.reshape(bsz, seq, GMLP_WIDTH)
    return gated @ w_out


def setup_inputs(seed: int = 0) -> dict:
    key = jax.random.key(seed)
    ks = jax.random.split(key, 32)
    f32 = jnp.float32
    nrm = lambda k, shape, scale: jax.random.normal(k, shape, f32) * scale
    x = jax.random.normal(ks[0], (BATCH, SEQ, D_MODEL), f32)
    ln_g = 1.0 + nrm(ks[1], (DEPTH, 3, D_MODEL), 0.02)
    ln_b = nrm(ks[2], (DEPTH, 3, D_MODEL), 0.02)
    ffn_w_in = nrm(ks[3], (DEPTH, 2, D_MODEL, 2 * D_FF), D_MODEL ** -0.5)
    ffn_w_out = nrm(ks[4], (DEPTH, 2, D_FF, D_MODEL), DN_BETA * D_FF ** -0.5)
    na = N_S5_LAYERS
    s5_w_in = nrm(ks[5], (na, D_MODEL, S5_WIDTH), D_MODEL ** -0.5)
    s5_a_re = -0.5 + nrm(ks[6], (na, S5_GROUPS, S5_STATE), 0.01)
    s5_a_im = (math.pi * jnp.arange(S5_STATE, dtype=f32))[None, None] + nrm(ks[7], (na, S5_GROUPS, S5_STATE), 0.01)
    s5_log_dt = jax.random.uniform(ks[8], (na, S5_GROUPS), f32, math.log(DT_MIN), math.log(DT_MAX))
    s5_b_re = nrm(ks[9], (na, S5_GROUPS, S5_STATE, S5_GROUP_CH), (2.0 * S5_GROUP_CH) ** -0.5)
    s5_b_im = nrm(ks[10], (na, S5_GROUPS, S5_STATE, S5_GROUP_CH), (2.0 * S5_GROUP_CH) ** -0.5)
    s5_c_re = nrm(ks[11], (na, S5_GROUPS, S5_GROUP_CH, S5_STATE), (2.0 * S5_STATE) ** -0.5)
    s5_c_im = nrm(ks[12], (na, S5_GROUPS, S5_GROUP_CH, S5_STATE), (2.0 * S5_STATE) ** -0.5)
    s5_d = nrm(ks[13], (na, S5_WIDTH), 1.0)
    s5_w_glu = nrm(ks[14], (na, S5_WIDTH, S5_WIDTH), S5_WIDTH ** -0.5)
    s5_b_glu = nrm(ks[15], (na, S5_WIDTH), 0.02)
    s5_w_out = nrm(ks[16], (na, S5_WIDTH, D_MODEL), DN_BETA * S5_WIDTH ** -0.5)
    nb = N_GMLP_LAYERS
    g_w_in = nrm(ks[17], (nb, D_MODEL, 2 * GMLP_WIDTH), D_MODEL ** -0.5)
    g_b_in = nrm(ks[18], (nb, 2 * GMLP_WIDTH), 0.02)
    g_ln_g = 1.0 + nrm(ks[19], (nb, GMLP_WIDTH), 0.02)
    g_ln_b = nrm(ks[20], (nb, GMLP_WIDTH), 0.02)
    g_w_s = nrm(ks[21], (nb, GMLP_HEADS, GMLP_BLOCK, GMLP_BLOCK), GMLP_BLOCK ** -0.5)
    g_b_s = 1.0 + nrm(ks[22], (nb, GMLP_HEADS, GMLP_BLOCK), 0.1)
    g_w_out = nrm(ks[23], (nb, GMLP_WIDTH, D_MODEL), DN_BETA * GMLP_WIDTH ** -0.5)
    return {
        "x": x, "ln_g": ln_g, "ln_b": ln_b, "ffn_w_in": ffn_w_in, "ffn_w_out": ffn_w_out,
        "s5_w_in": s5_w_in, "s5_a_re": s5_a_re, "s5_a_im": s5_a_im, "s5_log_dt": s5_log_dt,
        "s5_b_re": s5_b_re, "s5_b_im": s5_b_im, "s5_c_re": s5_c_re, "s5_c_im": s5_c_im,
        "s5_d": s5_d, "s5_w_glu": s5_w_glu, "s5_b_glu": s5_b_glu, "s5_w_out": s5_w_out,
        "g_w_in": g_w_in, "g_b_in": g_b_in, "g_ln_g": g_ln_g, "g_ln_b": g_ln_b,
        "g_w_s": g_w_s, "g_b_s": g_b_s, "g_w_out": g_w_out,
    }


def reference(x, ln_g, ln_b, ffn_w_in, ffn_w_out,
              s5_w_in, s5_a_re, s5_a_im, s5_log_dt, s5_b_re, s5_b_im, s5_c_re, s5_c_im,
              s5_d, s5_w_glu, s5_b_glu, s5_w_out,
              g_w_in, g_b_in, g_ln_g, g_ln_b, g_w_s, g_b_s, g_w_out):
    h = x
    for i in range(DEPTH):
        h = _layernorm(DN_ALPHA * h + 0.5 * _swiglu(h, ffn_w_in[i, 0], ffn_w_out[i, 0]), ln_g[i, 0], ln_b[i, 0])
        j = i // N_MIXERS
        if i % N_MIXERS == 0:
            mix = _s5_mixer(h, s5_w_in[j], s5_a_re[j], s5_a_im[j], s5_log_dt[j], s5_b_re[j], s5_b_im[j],
                            s5_c_re[j], s5_c_im[j], s5_d[j], s5_w_glu[j], s5_b_glu[j], s5_w_out[j])
        else:
            mix = _gmlp_mixer(h, g_w_in[j], g_b_in[j], g_ln_g[j], g_ln_b[j], g_w_s[j], g_b_s[j], g_w_out[j])
        h = _layernorm(DN_ALPHA * h + mix, ln_g[i, 1], ln_b[i, 1])
        h = _layernorm(DN_ALPHA * h + 0.5 * _swiglu(h, ffn_w_in[i, 1], ffn_w_out[i, 1]), ln_g[i, 2], ln_b[i, 2])
    return h
```

```python
import functools
import math

import jax
import jax.numpy as jnp
from jax import lax
from jax.experimental import pallas as pl
from jax.experimental.pallas import tpu as pltpu

D_MODEL = 1024
BATCH = 8
SEQ = 4096
DEPTH = 2
CHUNK = 64
D_FF = 2816
S5_GROUP_CH = 16
S5_GROUPS = D_MODEL // S5_GROUP_CH
S5_STATE = 64
GMLP_BLOCK = 128
GMLP_HEADS = 8
GMLP_HEAD_CH = D_MODEL // GMLP_HEADS
DN_ALPHA = (2.0 * DEPTH) ** 0.25
LN_EPS = 1e-5

LANES = 128
SUBLANES = 8
MXU_DIM = 256

N_TOKENS = BATCH * SEQ
FFN_ROWS = 512
FFN_COLS = MXU_DIM
FFN_CHUNKS = D_FF // FFN_COLS
S5_STEPS = 32
S5_ROWS = S5_STEPS * BATCH
S5_CHUNK_GROUPS = LANES // S5_GROUP_CH
S5_CHUNKS = S5_GROUPS // S5_CHUNK_GROUPS
S5_CHUNK_STATE = S5_CHUNK_GROUPS * S5_STATE
GMLP_ROWS = GMLP_BLOCK * BATCH
VMEM_LIMIT = 56 * 1024 * 1024

_F32 = jnp.float32
_BF16 = jnp.bfloat16


def _dot(a, b):
    return jnp.dot(a, b, preferred_element_type=_F32)


def _layernorm(y, g, b):
    mu = jnp.mean(y, axis=-1, keepdims=True)
    yc = y - mu
    var = jnp.mean(yc * yc, axis=-1, keepdims=True)
    return yc * lax.rsqrt(var + LN_EPS) * g + b


def _const_spec(shape):
    zeros = (0,) * len(shape)
    return pl.BlockSpec(shape, lambda i: zeros, pipeline_mode=pl.Buffered(1))


def _params():
    return pltpu.CompilerParams(
        dimension_semantics=("arbitrary",), vmem_limit_bytes=VMEM_LIMIT)


def _ffn_kernel(x_ref, w_in_ref, w_out_ref, g_ref, b_ref, o_ref):
    x = x_ref[...]
    xb = x.astype(_BF16)
    acc = jnp.zeros((FFN_ROWS, D_MODEL), _F32)
    for c in range(FFN_CHUNKS):
        gu = _dot(xb, w_in_ref[c])
        gate = gu[:, :FFN_COLS]
        up = gu[:, FFN_COLS:]
        h = (jax.nn.silu(gate) * up).astype(_BF16)
        acc = acc + _dot(h, w_out_ref[c])
    o_ref[...] = _layernorm(DN_ALPHA * x + 0.5 * acc, g_ref[...], b_ref[...])


def _ffn(x, w_in, w_out, g, b):
    n = x.shape[0]
    return pl.pallas_call(
        _ffn_kernel,
        grid=(n // FFN_ROWS,),
        in_specs=[
            pl.BlockSpec((FFN_ROWS, D_MODEL), lambda i: (i, 0)),
            _const_spec((FFN_CHUNKS, D_MODEL, 2 * FFN_COLS)),
            _const_spec((FFN_CHUNKS, FFN_COLS, D_MODEL)),
            _const_spec((1, D_MODEL)),
            _const_spec((1, D_MODEL)),
        ],
        out_specs=pl.BlockSpec((FFN_ROWS, D_MODEL), lambda i: (i, 0)),
        out_shape=jax.ShapeDtypeStruct((n, D_MODEL), _F32),
        compiler_params=_params(),
        name="ffn",
    )(x, w_in, w_out, g, b)


def _s5_kernel(x_ref, w_in_ref, bmat_ref, cmat_ref, lam_r_ref, lam_i_ref,
               d_ref, w_glu_ref, b_glu_ref, w_out_ref, g_ref, b_ref, o_ref,
               u_s, bu_s, y_s, hr_s, hi_s):
    @pl.when(pl.program_id(0) == 0)
    def _():
        hr_s[...] = jnp.zeros_like(hr_s)
        hi_s[...] = jnp.zeros_like(hi_s)

    x = x_ref[...]
    u_s[...] = _dot(x.astype(_BF16), w_in_ref[...])
    for j in range(S5_CHUNKS):
        cols = slice(j * LANES, (j + 1) * LANES)
        bu_s[...] = _dot(u_s[:, cols].astype(_BF16), bmat_ref[j])
        lam_r = lam_r_ref[j]
        lam_i = lam_i_ref[j]

        def step(t, carry):
            hr, hi = carry
            rows = pl.ds(pl.multiple_of(t * SUBLANES, SUBLANES), SUBLANES)
            br = bu_s[rows, :S5_CHUNK_STATE]
            bi = bu_s[rows, S5_CHUNK_STATE:]
            nhr = lam_r * hr - lam_i * hi + br
            nhi = lam_r * hi + lam_i * hr + bi
            bu_s[rows, :S5_CHUNK_STATE] = nhr
            bu_s[rows, S5_CHUNK_STATE:] = nhi
            return nhr, nhi

        hr, hi = lax.fori_loop(0, S5_STEPS, step, (hr_s[j], hi_s[j]), unroll=8)
        hr_s[j] = hr
        hi_s[j] = hi
        y_s[:, cols] = _dot(bu_s[...].astype(_BF16), cmat_ref[j])
    y = jax.nn.gelu(y_s[...] + d_ref[...] * u_s[...])
    gate = _dot(y.astype(_BF16), w_glu_ref[...]) + b_glu_ref[...]
    z = y * jax.nn.sigmoid(gate)
    mix = _dot(z.astype(_BF16), w_out_ref[...])
    o_ref[...] = _layernorm(DN_ALPHA * x + mix, g_ref[...], b_ref[...])


def _s5(x, w_in, bmat, cmat, lam_r, lam_i, d, w_glu, b_glu, w_out, g, b):
    n = x.shape[0]
    state = 2 * S5_CHUNK_STATE
    return pl.pallas_call(
        _s5_kernel,
        grid=(n // S5_ROWS,),
        in_specs=[
            pl.BlockSpec((S5_ROWS, D_MODEL), lambda i: (i, 0)),
            _const_spec((D_MODEL, D_MODEL)),
            _const_spec((S5_CHUNKS, LANES, state)),
            _const_spec((S5_CHUNKS, state, LANES)),
            _const_spec((S5_CHUNKS, SUBLANES, S5_CHUNK_STATE)),
            _const_spec((S5_CHUNKS, SUBLANES, S5_CHUNK_STATE)),
            _const_spec((1, D_MODEL)),
            _const_spec((D_MODEL, D_MODEL)),
            _const_spec((1, D_MODEL)),
            _const_spec((D_MODEL, D_MODEL)),
            _const_spec((1, D_MODEL)),
            _const_spec((1, D_MODEL)),
        ],
        out_specs=pl.BlockSpec((S5_ROWS, D_MODEL), lambda i: (i, 0)),
        out_shape=jax.ShapeDtypeStruct((n, D_MODEL), _F32),
        scratch_shapes=[
            pltpu.VMEM((S5_ROWS, D_MODEL), _F32),
            pltpu.VMEM((S5_ROWS, state), _F32),
            pltpu.VMEM((S5_ROWS, D_MODEL), _F32),
            pltpu.VMEM((S5_CHUNKS, SUBLANES, S5_CHUNK_STATE), _F32),
            pltpu.VMEM((S5_CHUNKS, SUBLANES, S5_CHUNK_STATE), _F32),
        ],
        compiler_params=_params(),
        name="s5",
    )(x, w_in, bmat, cmat, lam_r, lam_i, d, w_glu, b_glu, w_out, g, b)


def _s5_discretise(a_re, a_im, log_dt, b_re, b_im, c_re, c_im):
    dt = jnp.exp(log_dt)[:, None]
    mag = jnp.exp(a_re * dt)
    lam_r = mag * jnp.cos(a_im * dt)
    lam_i = mag * jnp.sin(a_im * dt)
    den = a_re * a_re + a_im * a_im
    coef_r = ((lam_r - 1.0) * a_re + lam_i * a_im) / den
    coef_i = (lam_i * a_re - (lam_r - 1.0) * a_im) / den
    bbar_r = coef_r[..., None] * b_re - coef_i[..., None] * b_im
    bbar_i = coef_r[..., None] * b_im + coef_i[..., None] * b_re
    k = S5_CHUNK_GROUPS
    eye = jnp.eye(k, dtype=_F32)

    def pack_b(bb):
        bb = bb.reshape(S5_CHUNKS, k, S5_STATE, S5_GROUP_CH)
        out = jnp.einsum('jgpc,gh->jgchp', bb, eye)
        return out.reshape(S5_CHUNKS, k * S5_GROUP_CH, k * S5_STATE)

    def pack_c(cc):
        cc = cc.reshape(S5_CHUNKS, k, S5_GROUP_CH, S5_STATE)
        out = jnp.einsum('jgcp,gh->jgphc', cc, eye)
        return out.reshape(S5_CHUNKS, k * S5_STATE, k * S5_GROUP_CH)

    bmat = jnp.concatenate([pack_b(bbar_r), pack_b(bbar_i)], axis=-1)
    cmat = jnp.concatenate([pack_c(c_re), pack_c(-c_im)], axis=1)

    def pack_lam(v):
        v = v.reshape(S5_CHUNKS, 1, S5_CHUNK_STATE)
        return jnp.broadcast_to(v, (S5_CHUNKS, SUBLANES, S5_CHUNK_STATE))

    return (bmat.astype(_BF16), cmat.astype(_BF16),
            pack_lam(lam_r), pack_lam(lam_i))


def _gmlp_kernel(x_ref, w_in_ref, b_in_ref, lg_ref, lb_ref, w_s_ref, b_s_ref,
                 w_out_ref, g_ref, b_ref, o_ref, v_s, s_s):
    x = x_ref[...]
    z = jax.nn.gelu(_dot(x.astype(_BF16), w_in_ref[...]) + b_in_ref[...])
    u = z[:, :D_MODEL]
    v = _layernorm(z[:, D_MODEL:], lg_ref[...], lb_ref[...])
    for h in range(GMLP_HEADS):
        v_s[h] = v[:, h * LANES:(h + 1) * LANES]
    pos_out = lax.broadcasted_iota(jnp.int32, (GMLP_BLOCK, GMLP_BLOCK), 0)
    pos_in = lax.broadcasted_iota(jnp.int32, (GMLP_BLOCK, GMLP_BLOCK), 1)
    mask = (pos_in // CHUNK) <= (pos_out // CHUNK)
    for h in range(GMLP_HEADS):
        w = jnp.where(mask, w_s_ref[h], 0.0).astype(_BF16)
        bias = b_s_ref[h]
        for b in range(BATCH):
            rows = pl.ds(b, GMLP_BLOCK, stride=BATCH)
            vb = v_s[h, rows, :].astype(_BF16)
            s_s[h, rows, :] = _dot(w, vb) + bias
    s = jnp.concatenate([s_s[h] for h in range(GMLP_HEADS)], axis=-1)
    mix = _dot((u * s).astype(_BF16), w_out_ref[...])
    o_ref[...] = _layernorm(DN_ALPHA * x + mix, g_ref[...], b_ref[...])


def _gmlp(x, w_in, b_in, lg, lb, w_s, b_s, w_out, g, b):
    n = x.shape[0]
    return pl.pallas_call(
        _gmlp_kernel,
        grid=(n // GMLP_ROWS,),
        in_specs=[
            pl.BlockSpec((GMLP_ROWS, D_MODEL), lambda i: (i, 0)),
            _const_spec((D_MODEL, 2 * D_MODEL)),
            _const_spec((1, 2 * D_MODEL)),
            _const_spec((1, D_MODEL)),
            _const_spec((1, D_MODEL)),
            _const_spec((GMLP_HEADS, GMLP_BLOCK, GMLP_BLOCK)),
            _const_spec((GMLP_HEADS, GMLP_BLOCK, LANES)),
            _const_spec((D_MODEL, D_MODEL)),
            _const_spec((1, D_MODEL)),
            _const_spec((1, D_MODEL)),
        ],
        out_specs=pl.BlockSpec((GMLP_ROWS, D_MODEL), lambda i: (i, 0)),
        out_shape=jax.ShapeDtypeStruct((n, D_MODEL), _F32),
        scratch_shapes=[
            pltpu.VMEM((GMLP_HEADS, GMLP_ROWS, LANES), _F32),
            pltpu.VMEM((GMLP_HEADS, GMLP_ROWS, LANES), _F32),
        ],
        compiler_params=_params(),
        name="gmlp",
    )(x, w_in, b_in, lg, lb, w_s, b_s, w_out, g, b)


def _pack_ffn(w_in, w_out):
    gate = w_in[:, :D_FF].reshape(D_MODEL, FFN_CHUNKS, FFN_COLS)
    up = w_in[:, D_FF:].reshape(D_MODEL, FFN_CHUNKS, FFN_COLS)
    w_in_p = jnp.concatenate([gate, up], axis=-1).transpose(1, 0, 2)
    w_out_p = w_out.reshape(FFN_CHUNKS, FFN_COLS, D_MODEL)
    return w_in_p.astype(_BF16), w_out_p.astype(_BF16)


def _row(v):
    return v.reshape(1, -1)


def kernel(x, ln_g, ln_b, ffn_w_in, ffn_w_out, s5_w_in, s5_a_re, s5_a_im, s5_log_dt, s5_b_re, s5_b_im, s5_c_re, s5_c_im, s5_d, s5_w_glu, s5_b_glu, s5_w_out, g_w_in, g_b_in, g_ln_g, g_ln_b, g_w_s, g_b_s, g_w_out):
    h = jnp.transpose(x, (1, 0, 2)).reshape(N_TOKENS, D_MODEL)
    for i in range(DEPTH):
        j = i // 2
        h = _ffn(h, *_pack_ffn(ffn_w_in[i, 0], ffn_w_out[i, 0]),
                 _row(ln_g[i, 0]), _row(ln_b[i, 0]))
        if i % 2 == 0:
            bmat, cmat, lam_r, lam_i = _s5_discretise(
                s5_a_re[j], s5_a_im[j], s5_log_dt[j], s5_b_re[j], s5_b_im[j],
                s5_c_re[j], s5_c_im[j])
            h = _s5(h, s5_w_in[j].astype(_BF16), bmat, cmat, lam_r, lam_i,
                    _row(s5_d[j]), s5_w_glu[j].astype(_BF16),
                    _row(s5_b_glu[j]), s5_w_out[j].astype(_BF16),
                    _row(ln_g[i, 1]), _row(ln_b[i, 1]))
        else:
            bias = jnp.broadcast_to(g_b_s[j][:, :, None],
                                    (GMLP_HEADS, GMLP_BLOCK, LANES))
            h = _gmlp(h, g_w_in[j].astype(_BF16), _row(g_b_in[j]),
                      _row(g_ln_g[j]), _row(g_ln_b[j]), g_w_s[j], bias,
                      g_w_out[j].astype(_BF16),
                      _row(ln_g[i, 1]), _row(ln_b[i, 1]))
        h = _ffn(h, *_pack_ffn(ffn_w_in[i, 1], ffn_w_out[i, 1]),
                 _row(ln_g[i, 2]), _row(ln_b[i, 2]))
    return jnp.transpose(h.reshape(SEQ, BATCH, D_MODEL), (1, 0, 2))
```

```python
import jax
import jax.numpy as jnp
from jax import lax
from jax.experimental import pallas as pl
from jax.experimental.pallas import tpu as pltpu

D_MODEL = 1024
BATCH = 8
SEQ = 4096
DEPTH = 2
CHUNK = 64
D_FF = 2816
S5_GROUP_CH = 16
S5_GROUPS = D_MODEL // S5_GROUP_CH
S5_STATE = 64
GMLP_BLOCK = 128
GMLP_HEADS = 8
DN_ALPHA = (2.0 * DEPTH) ** 0.25
LN_EPS = 1e-5

LANES = 128
SUBLANES = 8
MXU_DIM = 256

N_TOKENS = BATCH * SEQ
FFN_ROWS = 512
FFN_COLS = MXU_DIM
FFN_CHUNKS = D_FF // FFN_COLS
S5_STEPS = 128
S5_PAIRS = S5_STEPS // 2
S5_HALF = S5_PAIRS * BATCH
S5_CHUNK_GROUPS = LANES // S5_GROUP_CH
S5_CHUNKS = S5_GROUPS // S5_CHUNK_GROUPS
S5_CHUNK_STATE = S5_CHUNK_GROUPS * S5_STATE
S5_B_ROWS = 2 * LANES
S5_C_ROWS = 2 * S5_CHUNK_STATE + LANES
GMLP_ROWS = GMLP_BLOCK * BATCH
VMEM_LIMIT = 58 * 1024 * 1024

_F32 = jnp.float32
_BF16 = jnp.bfloat16


def _dot(a, b):
    return jnp.dot(a, b, preferred_element_type=_F32)


def _layernorm(y, g, b):
    mu = jnp.mean(y, axis=-1, keepdims=True)
    yc = y - mu
    var = jnp.mean(yc * yc, axis=-1, keepdims=True)
    return yc * lax.rsqrt(var + LN_EPS) * g + b


def _const_spec(shape, grid_rank=1):
    zeros = (0,) * len(shape)
    if grid_rank == 1:
        index_map = lambda i: zeros
    else:
        index_map = lambda i, j: zeros
    return pl.BlockSpec(shape, index_map, pipeline_mode=pl.Buffered(1))


def _params(grid_rank=1):
    return pltpu.CompilerParams(
        dimension_semantics=("arbitrary",) * grid_rank,
        vmem_limit_bytes=VMEM_LIMIT)


def _ffn_kernel(x_ref, w_in_ref, w_out_ref, g_ref, b_ref, o_ref):
    x = x_ref[...]
    xb = x.astype(_BF16)
    acc = jnp.zeros((FFN_ROWS, D_MODEL), _F32)
    for c in range(FFN_CHUNKS):
        gu = _dot(xb, w_in_ref[c])
        gate = gu[:, :FFN_COLS]
        up = gu[:, FFN_COLS:]
        h = (jax.nn.silu(gate) * up).astype(_BF16)
        acc = acc + _dot(h, w_out_ref[c])
    o_ref[...] = _layernorm(DN_ALPHA * x + 0.5 * acc, g_ref[...], b_ref[...])


_TIME_MAJOR = (SEQ, BATCH * D_MODEL)
_BATCH_MAJOR = (BATCH, SEQ, D_MODEL)


def _ffn(x, w_in, w_out, g, b, *, batch_major_in=False, batch_major_out=False):
    bm_spec = pl.BlockSpec((None, FFN_ROWS, D_MODEL), lambda i, j: (j, i, 0))
    tm_spec = pl.BlockSpec((FFN_ROWS, D_MODEL), lambda i, j: (i, j))
    return pl.pallas_call(
        _ffn_kernel,
        grid=(SEQ // FFN_ROWS, BATCH),
        in_specs=[
            bm_spec if batch_major_in else tm_spec,
            _const_spec((FFN_CHUNKS, D_MODEL, 2 * FFN_COLS), 2),
            _const_spec((FFN_CHUNKS, FFN_COLS, D_MODEL), 2),
            _const_spec((1, D_MODEL), 2),
            _const_spec((1, D_MODEL), 2),
        ],
        out_specs=bm_spec if batch_major_out else tm_spec,
        out_shape=jax.ShapeDtypeStruct(
            _BATCH_MAJOR if batch_major_out else _TIME_MAJOR, _F32),
        compiler_params=_params(2),
        name="ffn",
    )(x, w_in, w_out, g, b)


def _s5_kernel(x_ref, w_in_ref, bmat_ref, cmat_ref, lam_r_ref, lam_i_ref,
               d_ref, w_glu_ref, b_glu_ref, w_out_ref, g_ref, b_ref, o_ref,
               u_s, c_s, hb_s, y_s, hr_s, hi_s, ulast_s):
    @pl.when(pl.program_id(0) == 0)
    def _():
        hr_s[...] = jnp.zeros_like(hr_s)
        hi_s[...] = jnp.zeros_like(hi_s)
        ulast_s[...] = jnp.zeros_like(ulast_s)

    half = S5_HALF
    x_even = x_ref[:, 0].reshape(half, D_MODEL)
    x_odd = x_ref[:, 1].reshape(half, D_MODEL)
    xb = jnp.concatenate([x_even, x_odd], axis=0).astype(_BF16)
    u_s[...] = _dot(xb, w_in_ref[...])
    pair = 2 * SUBLANES
    for j in range(S5_CHUNKS):
        cols = slice(j * LANES, (j + 1) * LANES)
        buf = j % 2
        u_even = u_s[:half, cols]
        u_odd = u_s[half:, cols]
        u_prev = jnp.concatenate(
            [ulast_s[:, cols], u_odd[:half - SUBLANES]], axis=0)
        lhs = jnp.concatenate([u_even, u_prev], axis=1).astype(_BF16)
        c_s[buf] = _dot(lhs, bmat_ref[j])
        hb_s[buf, :, 2 * S5_CHUNK_STATE:] = u_odd.astype(_BF16)
        lam_r = lam_r_ref[j]
        lam_i = lam_i_ref[j]
        hr = hr_s[j]
        hi = hi_s[j]
        for t in range(S5_PAIRS // 2):
            hs_r, hs_i = [], []
            for k in range(2):
                rows = slice(t * pair + k * SUBLANES,
                             t * pair + (k + 1) * SUBLANES)
                cr = c_s[buf, rows, :S5_CHUNK_STATE]
                ci = c_s[buf, rows, S5_CHUNK_STATE:]
                hr, hi = (lam_r * hr - lam_i * hi + cr,
                          lam_r * hi + lam_i * hr + ci)
                hs_r.append(hr)
                hs_i.append(hi)
            rows2 = slice(t * pair, (t + 1) * pair)
            hb_s[buf, rows2, :S5_CHUNK_STATE] = (
                jnp.concatenate(hs_r, axis=0).astype(_BF16))
            hb_s[buf, rows2, S5_CHUNK_STATE:2 * S5_CHUNK_STATE] = (
                jnp.concatenate(hs_i, axis=0).astype(_BF16))
        hr_s[j] = hr
        hi_s[j] = hi
        yy = _dot(hb_s[buf], cmat_ref[j])
        y_s[:half, cols] = yy[:, :LANES]
        y_s[half:, cols] = yy[:, LANES:]
    ulast_s[...] = u_s[2 * half - SUBLANES:, :]
    y = jax.nn.gelu(y_s[...] + d_ref[...] * u_s[...])
    gate = _dot(y.astype(_BF16), w_glu_ref[...]) + b_glu_ref[...]
    z = y * jax.nn.sigmoid(gate)
    mix = _dot(z.astype(_BF16), w_out_ref[...])
    g = g_ref[...]
    b = b_ref[...]
    out_even = _layernorm(DN_ALPHA * x_even + mix[:half], g, b)
    out_odd = _layernorm(DN_ALPHA * x_odd + mix[half:], g, b)
    o_ref[:, 0] = out_even.reshape(S5_PAIRS, BATCH, D_MODEL)
    o_ref[:, 1] = out_odd.reshape(S5_PAIRS, BATCH, D_MODEL)


def _s5(x, w_in, bmat, cmat, lam_r, lam_i, d, w_glu, b_glu, w_out, g, b):
    state = 2 * S5_CHUNK_STATE
    block = (S5_PAIRS, 2, BATCH, D_MODEL)
    return pl.pallas_call(
        _s5_kernel,
        grid=(SEQ // S5_STEPS,),
        in_specs=[
            pl.BlockSpec(block, lambda i: (i, 0, 0, 0)),
            _const_spec((D_MODEL, D_MODEL)),
            _const_spec((S5_CHUNKS, S5_B_ROWS, state)),
            _const_spec((S5_CHUNKS, S5_C_ROWS, 2 * LANES)),
            _const_spec((S5_CHUNKS, SUBLANES, S5_CHUNK_STATE)),
            _const_spec((S5_CHUNKS, SUBLANES, S5_CHUNK_STATE)),
            _const_spec((1, D_MODEL)),
            _const_spec((D_MODEL, D_MODEL)),
            _const_spec((1, D_MODEL)),
            _const_spec((D_MODEL, D_MODEL)),
            _const_spec((1, D_MODEL)),
            _const_spec((1, D_MODEL)),
        ],
        out_specs=pl.BlockSpec(block, lambda i: (i, 0, 0, 0)),
        out_shape=jax.ShapeDtypeStruct(x.shape, _F32),
        scratch_shapes=[
            pltpu.VMEM((2 * S5_HALF, D_MODEL), _F32),
            pltpu.VMEM((2, S5_HALF, state), _F32),
            pltpu.VMEM((2, S5_HALF, S5_C_ROWS), _BF16),
            pltpu.VMEM((2 * S5_HALF, D_MODEL), _F32),
            pltpu.VMEM((S5_CHUNKS, SUBLANES, S5_CHUNK_STATE), _F32),
            pltpu.VMEM((S5_CHUNKS, SUBLANES, S5_CHUNK_STATE), _F32),
            pltpu.VMEM((SUBLANES, D_MODEL), _F32),
        ],
        compiler_params=_params(),
        name="s5",
    )(x, w_in, bmat, cmat, lam_r, lam_i, d, w_glu, b_glu, w_out, g, b)


def _s5_discretise(a_re, a_im, log_dt, b_re, b_im, c_re, c_im):
    hi = lax.Precision.HIGHEST
    dt = jnp.exp(log_dt)[:, None]
    mag = jnp.exp(a_re * dt)
    lam_r = mag * jnp.cos(a_im * dt)
    lam_i = mag * jnp.sin(a_im * dt)
    den = a_re * a_re + a_im * a_im
    coef_r = ((lam_r - 1.0) * a_re + lam_i * a_im) / den
    coef_i = (lam_i * a_re - (lam_r - 1.0) * a_im) / den
    bbar_r = coef_r[..., None] * b_re - coef_i[..., None] * b_im
    bbar_i = coef_r[..., None] * b_im + coef_i[..., None] * b_re
    lr, li = lam_r[..., None], lam_i[..., None]
    lb_r = lr * bbar_r - li * bbar_i
    lb_i = lr * bbar_i + li * bbar_r
    lr, li = lam_r[:, None, :], lam_i[:, None, :]
    cl_r = c_re * lr - c_im * li
    cl_i = c_re * li + c_im * lr
    cb = (jnp.einsum('gcp,gpd->gcd', c_re, bbar_r, precision=hi)
          - jnp.einsum('gcp,gpd->gcd', c_im, bbar_i, precision=hi))
    k = S5_CHUNK_GROUPS
    eye = jnp.eye(k, dtype=_F32)

    def pack_b(bb):
        bb = bb.reshape(S5_CHUNKS, k, S5_STATE, S5_GROUP_CH)
        out = jnp.einsum('jgpc,gh->jgchp', bb, eye, precision=hi)
        return out.reshape(S5_CHUNKS, k * S5_GROUP_CH, k * S5_STATE)

    def pack_c(cc):
        cc = cc.reshape(S5_CHUNKS, k, S5_GROUP_CH, S5_STATE)
        out = jnp.einsum('jgcp,gh->jgphc', cc, eye, precision=hi)
        return out.reshape(S5_CHUNKS, k * S5_STATE, k * S5_GROUP_CH)

    def pack_e(ee):
        ee = ee.reshape(S5_CHUNKS, k, S5_GROUP_CH, S5_GROUP_CH)
        out = jnp.einsum('jgcd,gh->jgdhc', ee, eye, precision=hi)
        return out.reshape(S5_CHUNKS, k * S5_GROUP_CH, k * S5_GROUP_CH)

    bmat = jnp.concatenate([
        jnp.concatenate([pack_b(bbar_r), pack_b(bbar_i)], axis=-1),
        jnp.concatenate([pack_b(lb_r), pack_b(lb_i)], axis=-1)], axis=1)
    c_even = jnp.concatenate(
        [pack_c(c_re), pack_c(-c_im), jnp.zeros((S5_CHUNKS, LANES, LANES), _F32)],
        axis=1)
    c_odd = jnp.concatenate([pack_c(cl_r), pack_c(-cl_i), pack_e(cb)], axis=1)
    cmat = jnp.concatenate([c_even, c_odd], axis=-1)

    def pack_lam(v):
        v = v.reshape(S5_CHUNKS, 1, S5_CHUNK_STATE)
        return jnp.broadcast_to(v, (S5_CHUNKS, SUBLANES, S5_CHUNK_STATE))

    lam2_r = lam_r * lam_r - lam_i * lam_i
    lam2_i = 2.0 * lam_r * lam_i
    return (bmat.astype(_BF16), cmat.astype(_BF16),
            pack_lam(lam2_r), pack_lam(lam2_i))


def _gmlp_kernel(x_ref, w_in_ref, b_in_ref, lg_ref, lb_ref, w_s_ref, b_s_ref,
                 w_out_ref, g_ref, b_ref, o_ref):
    x = jnp.concatenate(
        [x_ref[:, b * D_MODEL:(b + 1) * D_MODEL] for b in range(BATCH)], axis=0)
    z = jax.nn.gelu(_dot(x.astype(_BF16), w_in_ref[...]) + b_in_ref[...])
    u = z[:, :D_MODEL]
    v = _layernorm(z[:, D_MODEL:], lg_ref[...], lb_ref[...]).astype(_BF16)
    pos_out = lax.broadcasted_iota(jnp.int32, (GMLP_BLOCK, GMLP_BLOCK), 0)
    pos_in = lax.broadcasted_iota(jnp.int32, (GMLP_BLOCK, GMLP_BLOCK), 1)
    mask = (pos_in // CHUNK) <= (pos_out // CHUNK)
    s_heads = []
    for h in range(GMLP_HEADS):
        w = jnp.where(mask, w_s_ref[h], 0.0).astype(_BF16)
        hcols = slice(h * LANES, (h + 1) * LANES)
        vh = jnp.concatenate(
            [v[b * GMLP_BLOCK:(b + 1) * GMLP_BLOCK, hcols]
             for b in range(BATCH)], axis=1)
        bias = jnp.concatenate([b_s_ref[h]] * BATCH, axis=1)
        sh = _dot(w, vh) + bias
        s_heads.append(jnp.concatenate(
            [sh[:, b * LANES:(b + 1) * LANES] for b in range(BATCH)], axis=0))
    s = jnp.concatenate(s_heads, axis=1)
    mix = _dot((u * s).astype(_BF16), w_out_ref[...])
    out = _layernorm(DN_ALPHA * x + mix, g_ref[...], b_ref[...])
    for b in range(BATCH):
        o_ref[:, b * D_MODEL:(b + 1) * D_MODEL] = (
            out[b * GMLP_BLOCK:(b + 1) * GMLP_BLOCK])


def _gmlp(x, w_in, b_in, lg, lb, w_s, b_s, w_out, g, b):
    block = (GMLP_BLOCK, BATCH * D_MODEL)
    return pl.pallas_call(
        _gmlp_kernel,
        grid=(SEQ // GMLP_BLOCK,),
        in_specs=[
            pl.BlockSpec(block, lambda i: (i, 0)),
            _const_spec((D_MODEL, 2 * D_MODEL)),
            _const_spec((1, 2 * D_MODEL)),
            _const_spec((1, D_MODEL)),
            _const_spec((1, D_MODEL)),
            _const_spec((GMLP_HEADS, GMLP_BLOCK, GMLP_BLOCK)),
            _const_spec((GMLP_HEADS, GMLP_BLOCK, LANES)),
            _const_spec((D_MODEL, D_MODEL)),
            _const_spec((1, D_MODEL)),
            _const_spec((1, D_MODEL)),
        ],
        out_specs=pl.BlockSpec(block, lambda i: (i, 0)),
        out_shape=jax.ShapeDtypeStruct(_TIME_MAJOR, _F32),
        compiler_params=_params(),
        name="gmlp",
    )(x, w_in, b_in, lg, lb, w_s, b_s, w_out, g, b)


def _pack_ffn(w_in, w_out):
    gate = w_in[:, :D_FF].reshape(D_MODEL, FFN_CHUNKS, FFN_COLS)
    up = w_in[:, D_FF:].reshape(D_MODEL, FFN_CHUNKS, FFN_COLS)
    w_in_p = jnp.concatenate([gate, up], axis=-1).transpose(1, 0, 2)
    w_out_p = w_out.reshape(FFN_CHUNKS, FFN_COLS, D_MODEL)
    return w_in_p.astype(_BF16), w_out_p.astype(_BF16)


def _row(v):
    return v.reshape(1, -1)


def kernel(x, ln_g, ln_b, ffn_w_in, ffn_w_out, s5_w_in, s5_a_re, s5_a_im, s5_log_dt, s5_b_re, s5_b_im, s5_c_re, s5_c_im, s5_d, s5_w_glu, s5_b_glu, s5_w_out, g_w_in, g_b_in, g_ln_g, g_ln_b, g_w_s, g_b_s, g_w_out):
    h = x
    for i in range(DEPTH):
        j = i // 2
        h = _ffn(h, *_pack_ffn(ffn_w_in[i, 0], ffn_w_out[i, 0]),
                 _row(ln_g[i, 0]), _row(ln_b[i, 0]), batch_major_in=(i == 0))
        if i % 2 == 0:
            bmat, cmat, lam_r, lam_i = _s5_discretise(
                s5_a_re[j], s5_a_im[j], s5_log_dt[j], s5_b_re[j], s5_b_im[j],
                s5_c_re[j], s5_c_im[j])
            h = h.reshape(SEQ // 2, 2, BATCH, D_MODEL)
            h = _s5(h, s5_w_in[j].astype(_BF16), bmat, cmat, lam_r, lam_i,
                    _row(s5_d[j]), s5_w_glu[j].astype(_BF16),
                    _row(s5_b_glu[j]), s5_w_out[j].astype(_BF16),
                    _row(ln_g[i, 1]), _row(ln_b[i, 1]))
            h = h.reshape(_TIME_MAJOR)
        else:
            bias = jnp.broadcast_to(g_b_s[j][:, :, None],
                                    (GMLP_HEADS, GMLP_BLOCK, LANES))
            h = _gmlp(h, g_w_in[j].astype(_BF16), _row(g_b_in[j]),
                      _row(g_ln_g[j]), _row(g_ln_b[j]), g_w_s[j], bias,
                      g_w_out[j].astype(_BF16),
                      _row(ln_g[i, 1]), _row(ln_b[i, 1]))
        h = _ffn(h, *_pack_ffn(ffn_w_in[i, 1], ffn_w_out[i, 1]),
                 _row(ln_g[i, 2]), _row(ln_b[i, 2]),
                 batch_major_out=(i == DEPTH - 1))
    return h
```

```python
import functools

import jax
import jax.numpy as jnp
from jax import lax
from jax.experimental import pallas as pl
from jax.experimental.pallas import tpu as pltpu

D_MODEL = 1024
BATCH = 8
SEQ = 4096
DEPTH = 2
CHUNK = 64
D_FF = 2816
S5_GROUP_CH = 16
S5_GROUPS = D_MODEL // S5_GROUP_CH
S5_STATE = 64
GMLP_BLOCK = 128
GMLP_HEADS = 8
DN_ALPHA = (2.0 * DEPTH) ** 0.25
LN_EPS = 1e-5

LANES = 128
SUBLANES = 8
MXU_DIM = 256

N_TOKENS = BATCH * SEQ
FFN_ROWS = 512
FFN_T = FFN_ROWS // BATCH
FFN_COLS = MXU_DIM
FFN_CHUNKS = D_FF // FFN_COLS
S5_STEPS = 128
S5_PAIRS = S5_STEPS // 2
S5_HALF = S5_PAIRS * BATCH
S5_CHUNK_GROUPS = LANES // S5_GROUP_CH
S5_CHUNKS = S5_GROUPS // S5_CHUNK_GROUPS
S5_CHUNK_STATE = S5_CHUNK_GROUPS * S5_STATE
S5_B_ROWS = 2 * LANES
S5_C_ROWS = 2 * S5_CHUNK_STATE + LANES
GMLP_ROWS = GMLP_BLOCK * BATCH
VMEM_LIMIT = 58 * 1024 * 1024

_F32 = jnp.float32
_BF16 = jnp.bfloat16


def _dot(a, b):
    return jnp.dot(a, b, preferred_element_type=_F32)


def _layernorm(y, g, b):
    mu = jnp.mean(y, axis=-1, keepdims=True)
    yc = y - mu
    var = jnp.mean(yc * yc, axis=-1, keepdims=True)
    return yc * lax.rsqrt(var + LN_EPS) * g + b


def _const_spec(shape, grid_rank=1):
    zeros = (0,) * len(shape)
    if grid_rank == 1:
        index_map = lambda i: zeros
    else:
        index_map = lambda i, j: zeros
    return pl.BlockSpec(shape, index_map, pipeline_mode=pl.Buffered(1))


def _params(grid_rank=1):
    return pltpu.CompilerParams(
        dimension_semantics=("arbitrary",) * grid_rank,
        vmem_limit_bytes=VMEM_LIMIT)


_BATCH = (BATCH, SEQ, D_MODEL)
_TALL = (SEQ * BATCH, D_MODEL)
_WIDE = (SEQ, BATCH * D_MODEL)


def _ffn_core(x, w_in_ref, w_out_ref, g, b):
    xb = x.astype(_BF16)
    acc = jnp.zeros((FFN_ROWS, D_MODEL), _F32)
    for c in range(FFN_CHUNKS):
        cols = slice(c * FFN_COLS, (c + 1) * FFN_COLS)
        up_cols = slice(D_FF + c * FFN_COLS, D_FF + (c + 1) * FFN_COLS)
        gate = _dot(xb, w_in_ref[:, cols])
        up = _dot(xb, w_in_ref[:, up_cols])
        h = (jax.nn.silu(gate) * up).astype(_BF16)
        acc = acc + _dot(h, w_out_ref[cols, :])
    return _layernorm(DN_ALPHA * x + 0.5 * acc, g, b)


def _ffn_kernel(src, dst, x_ref, w_in_ref, w_out_ref, g_ref, b_ref, o_ref,
                *scratch):
    g = g_ref[...]
    b = b_ref[...]
    slabs = range(D_MODEL // LANES)
    for half in range(2):
        rows = slice(half * FFN_ROWS, (half + 1) * FFN_ROWS)
        steps = slice(half * FFN_T, (half + 1) * FFN_T)
        if src == "batch":
            x = x_ref[:, steps, :].reshape(FFN_ROWS, D_MODEL)
        else:
            x = x_ref[rows, :]
        res = _ffn_core(x, w_in_ref, w_out_ref, g, b)
        if src == "batch" and dst == "tall":
            slab = scratch[0].at[half]
            for s in slabs:
                for bb in range(BATCH):
                    slab[s, pl.ds(bb, FFN_T, stride=BATCH), :] = (
                        res[bb * FFN_T:(bb + 1) * FFN_T,
                            s * LANES:(s + 1) * LANES])
            for s in slabs:
                o_ref[rows, s * LANES:(s + 1) * LANES] = slab[s]
        elif src == "tall" and dst == "wide":
            slab = scratch[0].at[half]
            for s in slabs:
                slab[s] = res[:, s * LANES:(s + 1) * LANES]
            for bb in range(BATCH):
                for s in slabs:
                    col = bb * D_MODEL + s * LANES
                    o_ref[steps, col:col + LANES] = (
                        slab[s, pl.ds(bb, FFN_T, stride=BATCH), :])
        else:
            o_ref[rows, :] = res


def _ffn(x, w_in, w_out, ln_g, ln_b, layer, slot, *, src, dst):
    assert (src, dst) in {("batch", "tall"), ("tall", "wide"),
                          ("wide", "wide"), ("wide", "batch")}, (src, dst)
    tile = 2 * FFN_ROWS
    if src == "wide":
        grid = (SEQ // tile, BATCH)
        wide_spec = pl.BlockSpec((tile, D_MODEL), lambda i, j: (i, j))
        in_spec = wide_spec
        out_spec = (wide_spec if dst == "wide" else
                    pl.BlockSpec((None, tile, D_MODEL), lambda i, j: (j, i, 0)))
        scratch = []
    else:
        grid = (SEQ // (2 * FFN_T),)
        tall_spec = pl.BlockSpec((tile, D_MODEL), lambda i: (i, 0))
        in_spec = (pl.BlockSpec((BATCH, 2 * FFN_T, D_MODEL),
                                lambda i: (0, i, 0))
                   if src == "batch" else tall_spec)
        out_spec = (tall_spec if dst == "tall" else
                    pl.BlockSpec((2 * FFN_T, BATCH * D_MODEL),
                                 lambda i: (i, 0)))
        scratch = [pltpu.VMEM((2, D_MODEL // LANES, FFN_ROWS, LANES), _F32)]
    rank = len(grid)

    def stacked(shape, *lead):
        index = lead + (0,) * len(shape)
        index_map = (lambda i: index) if rank == 1 else (lambda i, j: index)
        return pl.BlockSpec((None,) * len(lead) + shape, index_map,
                            pipeline_mode=pl.Buffered(1))

    out_shape = {"batch": _BATCH, "tall": _TALL, "wide": _WIDE}[dst]
    return pl.pallas_call(
        functools.partial(_ffn_kernel, src, dst),
        grid=grid,
        in_specs=[
            in_spec,
            stacked((D_MODEL, 2 * D_FF), layer, slot),
            stacked((D_FF, D_MODEL), layer, slot),
            stacked((1, D_MODEL), 3 * layer + 2 * slot),
            stacked((1, D_MODEL), 3 * layer + 2 * slot),
        ],
        out_specs=out_spec,
        out_shape=jax.ShapeDtypeStruct(out_shape, _F32),
        scratch_shapes=scratch,
        compiler_params=_params(rank),
        name="ffn",
    )(x, w_in, w_out, ln_g, ln_b)


def _s5_kernel(x_ref, w_in_ref, bmat_ref, cmat_ref, lam_r_ref, lam_i_ref,
               d_ref, w_glu_ref, b_glu_ref, w_out_ref, g_ref, b_ref, o_ref,
               u_s, c_s, hb_s, y_s, hr_s, hi_s, ulast_s):
    @pl.when(pl.program_id(0) == 0)
    def _():
        hr_s[...] = jnp.zeros_like(hr_s)
        hi_s[...] = jnp.zeros_like(hi_s)
        ulast_s[...] = jnp.zeros_like(ulast_s)

    half = S5_HALF
    x_even = x_ref[:, 0].reshape(half, D_MODEL)
    x_odd = x_ref[:, 1].reshape(half, D_MODEL)
    xb = jnp.concatenate([x_even, x_odd], axis=0).astype(_BF16)
    u_s[...] = _dot(xb, w_in_ref[...])
    pair = 2 * SUBLANES
    for j in range(S5_CHUNKS):
        cols = slice(j * LANES, (j + 1) * LANES)
        buf = j % 2
        u_even = u_s[:half, cols]
        u_odd = u_s[half:, cols]
        u_prev = jnp.concatenate(
            [ulast_s[:, cols], u_odd[:half - SUBLANES]], axis=0)
        lhs = jnp.concatenate([u_even, u_prev], axis=1).astype(_BF16)
        c_s[buf] = _dot(lhs, bmat_ref[j])
        hb_s[buf, :, 2 * S5_CHUNK_STATE:] = u_odd.astype(_BF16)
        lam_r = lam_r_ref[j]
        lam_i = lam_i_ref[j]
        hr = hr_s[j]
        hi = hi_s[j]
        for t in range(S5_PAIRS // 2):
            hs_r, hs_i = [], []
            for k in range(2):
                rows = slice(t * pair + k * SUBLANES,
                             t * pair + (k + 1) * SUBLANES)
                cr = c_s[buf, rows, :S5_CHUNK_STATE]
                ci = c_s[buf, rows, S5_CHUNK_STATE:]
                hr, hi = (lam_r * hr - lam_i * hi + cr,
                          lam_r * hi + lam_i * hr + ci)
                hs_r.append(hr)
                hs_i.append(hi)
            rows2 = slice(t * pair, (t + 1) * pair)
            hb_s[buf, rows2, :S5_CHUNK_STATE] = (
                jnp.concatenate(hs_r, axis=0).astype(_BF16))
            hb_s[buf, rows2, S5_CHUNK_STATE:2 * S5_CHUNK_STATE] = (
                jnp.concatenate(hs_i, axis=0).astype(_BF16))
        hr_s[j] = hr
        hi_s[j] = hi
        yy = _dot(hb_s[buf], cmat_ref[j])
        y_s[:half, cols] = yy[:, :LANES]
        y_s[half:, cols] = yy[:, LANES:]
    ulast_s[...] = u_s[2 * half - SUBLANES:, :]
    y = jax.nn.gelu(y_s[...] + d_ref[...] * u_s[...])
    gate = _dot(y.astype(_BF16), w_glu_ref[...]) + b_glu_ref[...]
    z = y * jax.nn.sigmoid(gate)
    mix = _dot(z.astype(_BF16), w_out_ref[...])
    g = g_ref[...]
    b = b_ref[...]
    out_even = _layernorm(DN_ALPHA * x_even + mix[:half], g, b)
    out_odd = _layernorm(DN_ALPHA * x_odd + mix[half:], g, b)
    o_ref[:, 0] = out_even.reshape(S5_PAIRS, BATCH, D_MODEL)
    o_ref[:, 1] = out_odd.reshape(S5_PAIRS, BATCH, D_MODEL)


def _s5(x, w_in, bmat, cmat, lam_r, lam_i, d, w_glu, b_glu, w_out, g, b):
    state = 2 * S5_CHUNK_STATE
    block = (S5_PAIRS, 2, BATCH, D_MODEL)
    return pl.pallas_call(
        _s5_kernel,
        grid=(SEQ // S5_STEPS,),
        in_specs=[
            pl.BlockSpec(block, lambda i: (i, 0, 0, 0)),
            _const_spec((D_MODEL, D_MODEL)),
            _const_spec((S5_CHUNKS, S5_B_ROWS, state)),
            _const_spec((S5_CHUNKS, S5_C_ROWS, 2 * LANES)),
            _const_spec((S5_CHUNKS, SUBLANES, S5_CHUNK_STATE)),
            _const_spec((S5_CHUNKS, SUBLANES, S5_CHUNK_STATE)),
            _const_spec((1, D_MODEL)),
            _const_spec((D_MODEL, D_MODEL)),
            _const_spec((1, D_MODEL)),
            _const_spec((D_MODEL, D_MODEL)),
            _const_spec((1, D_MODEL)),
            _const_spec((1, D_MODEL)),
        ],
        out_specs=pl.BlockSpec(block, lambda i: (i, 0, 0, 0)),
        out_shape=jax.ShapeDtypeStruct(x.shape, _F32),
        scratch_shapes=[
            pltpu.VMEM((2 * S5_HALF, D_MODEL), _F32),
            pltpu.VMEM((2, S5_HALF, state), _F32),
            pltpu.VMEM((2, S5_HALF, S5_C_ROWS), _BF16),
            pltpu.VMEM((2 * S5_HALF, D_MODEL), _F32),
            pltpu.VMEM((S5_CHUNKS, SUBLANES, S5_CHUNK_STATE), _F32),
            pltpu.VMEM((S5_CHUNKS, SUBLANES, S5_CHUNK_STATE), _F32),
            pltpu.VMEM((SUBLANES, D_MODEL), _F32),
        ],
        compiler_params=_params(),
        name="s5",
    )(x, w_in, bmat, cmat, lam_r, lam_i, d, w_glu, b_glu, w_out, g, b)


def _s5_discretise(a_re, a_im, log_dt, b_re, b_im, c_re, c_im):
    hi = lax.Precision.HIGHEST
    dt = jnp.exp(log_dt)[:, None]
    mag = jnp.exp(a_re * dt)
    lam_r = mag * jnp.cos(a_im * dt)
    lam_i = mag * jnp.sin(a_im * dt)
    den = a_re * a_re + a_im * a_im
    coef_r = ((lam_r - 1.0) * a_re + lam_i * a_im) / den
    coef_i = (lam_i * a_re - (lam_r - 1.0) * a_im) / den
    bbar_r = coef_r[..., None] * b_re - coef_i[..., None] * b_im
    bbar_i = coef_r[..., None] * b_im + coef_i[..., None] * b_re
    lr, li = lam_r[..., None], lam_i[..., None]
    lb_r = lr * bbar_r - li * bbar_i
    lb_i = lr * bbar_i + li * bbar_r
    lr, li = lam_r[:, None, :], lam_i[:, None, :]
    cl_r = c_re * lr - c_im * li
    cl_i = c_re * li + c_im * lr
    cb = (jnp.einsum('gcp,gpd->gcd', c_re, bbar_r, precision=hi)
          - jnp.einsum('gcp,gpd->gcd', c_im, bbar_i, precision=hi))
    k = S5_CHUNK_GROUPS
    eye = jnp.eye(k, dtype=_F32)

    def pack_b(bb):
        bb = bb.reshape(S5_CHUNKS, k, S5_STATE, S5_GROUP_CH)
        out = jnp.einsum('jgpc,gh->jgchp', bb, eye, precision=hi)
        return out.reshape(S5_CHUNKS, k * S5_GROUP_CH, k * S5_STATE)

    def pack_c(cc):
        cc = cc.reshape(S5_CHUNKS, k, S5_GROUP_CH, S5_STATE)
        out = jnp.einsum('jgcp,gh->jgphc', cc, eye, precision=hi)
        return out.reshape(S5_CHUNKS, k * S5_STATE, k * S5_GROUP_CH)

    def pack_e(ee):
        ee = ee.reshape(S5_CHUNKS, k, S5_GROUP_CH, S5_GROUP_CH)
        out = jnp.einsum('jgcd,gh->jgdhc', ee, eye, precision=hi)
        return out.reshape(S5_CHUNKS, k * S5_GROUP_CH, k * S5_GROUP_CH)

    bmat = jnp.concatenate([
        jnp.concatenate([pack_b(bbar_r), pack_b(bbar_i)], axis=-1),
        jnp.concatenate([pack_b(lb_r), pack_b(lb_i)], axis=-1)], axis=1)
    c_even = jnp.concatenate(
        [pack_c(c_re), pack_c(-c_im), jnp.zeros((S5_CHUNKS, LANES, LANES), _F32)],
        axis=1)
    c_odd = jnp.concatenate([pack_c(cl_r), pack_c(-cl_i), pack_e(cb)], axis=1)
    cmat = jnp.concatenate([c_even, c_odd], axis=-1)

    def pack_lam(v):
        v = v.reshape(S5_CHUNKS, 1, S5_CHUNK_STATE)
        return jnp.broadcast_to(v, (S5_CHUNKS, SUBLANES, S5_CHUNK_STATE))

    lam2_r = lam_r * lam_r - lam_i * lam_i
    lam2_i = 2.0 * lam_r * lam_i
    return (bmat.astype(_BF16), cmat.astype(_BF16),
            pack_lam(lam2_r), pack_lam(lam2_i))


def _gmlp_kernel(x_ref, w_in_ref, b_in_ref, lg_ref, lb_ref, w_s_ref, b_s_ref,
                 w_out_ref, g_ref, b_ref, o_ref):
    x = jnp.concatenate(
        [x_ref[:, b * D_MODEL:(b + 1) * D_MODEL] for b in range(BATCH)], axis=0)
    z = jax.nn.gelu(_dot(x.astype(_BF16), w_in_ref[...]) + b_in_ref[...])
    u = z[:, :D_MODEL]
    v = _layernorm(z[:, D_MODEL:], lg_ref[...], lb_ref[...]).astype(_BF16)
    pos_out = lax.broadcasted_iota(jnp.int32, (GMLP_BLOCK, GMLP_BLOCK), 0)
    pos_in = lax.broadcasted_iota(jnp.int32, (GMLP_BLOCK, GMLP_BLOCK), 1)
    mask = (pos_in // CHUNK) <= (pos_out // CHUNK)
    s_heads = []
    for h in range(GMLP_HEADS):
        w = jnp.where(mask, w_s_ref[h], 0.0).astype(_BF16)
        hcols = slice(h * LANES, (h + 1) * LANES)
        vh = jnp.concatenate(
            [v[b * GMLP_BLOCK:(b + 1) * GMLP_BLOCK, hcols]
             for b in range(BATCH)], axis=1)
        bias = jnp.concatenate([b_s_ref[h]] * BATCH, axis=1)
        sh = _dot(w, vh) + bias
        s_heads.append(jnp.concatenate(
            [sh[:, b * LANES:(b + 1) * LANES] for b in range(BATCH)], axis=0))
    s = jnp.concatenate(s_heads, axis=1)
    mix = _dot((u * s).astype(_BF16), w_out_ref[...])
    out = _layernorm(DN_ALPHA * x + mix, g_ref[...], b_ref[...])
    for b in range(BATCH):
        o_ref[:, b * D_MODEL:(b + 1) * D_MODEL] = (
            out[b * GMLP_BLOCK:(b + 1) * GMLP_BLOCK])


def _gmlp(x, w_in, b_in, lg, lb, w_s, b_s, w_out, g, b):
    block = (GMLP_BLOCK, BATCH * D_MODEL)
    return pl.pallas_call(
        _gmlp_kernel,
        grid=(SEQ // GMLP_BLOCK,),
        in_specs=[
            pl.BlockSpec(block, lambda i: (i, 0)),
            _const_spec((D_MODEL, 2 * D_MODEL)),
            _const_spec((1, 2 * D_MODEL)),
            _const_spec((1, D_MODEL)),
            _const_spec((1, D_MODEL)),
            _const_spec((GMLP_HEADS, GMLP_BLOCK, GMLP_BLOCK)),
            _const_spec((GMLP_HEADS, GMLP_BLOCK, LANES)),
            _const_spec((D_MODEL, D_MODEL)),
            _const_spec((1, D_MODEL)),
            _const_spec((1, D_MODEL)),
        ],
        out_specs=pl.BlockSpec(block, lambda i: (i, 0)),
        out_shape=jax.ShapeDtypeStruct(_WIDE, _F32),
        compiler_params=_params(),
        name="gmlp",
    )(x, w_in, b_in, lg, lb, w_s, b_s, w_out, g, b)


def _row(v):
    return v.reshape(1, -1)


def kernel(x, ln_g, ln_b, ffn_w_in, ffn_w_out, s5_w_in, s5_a_re, s5_a_im, s5_log_dt, s5_b_re, s5_b_im, s5_c_re, s5_c_im, s5_d, s5_w_glu, s5_b_glu, s5_w_out, g_w_in, g_b_in, g_ln_g, g_ln_b, g_w_s, g_b_s, g_w_out):
    assert DEPTH == 2, "layout chain below is batch -> tall -> wide -> batch"
    w_in = ffn_w_in.astype(_BF16)
    w_out = ffn_w_out.astype(_BF16)
    ln_g3 = ln_g.reshape(DEPTH * 3, 1, D_MODEL)
    ln_b3 = ln_b.reshape(DEPTH * 3, 1, D_MODEL)
    h = x
    layout = "batch"
    for i in range(DEPTH):
        j = i // 2
        mixer_layout = "tall" if i % 2 == 0 else "wide"
        h = _ffn(h, w_in, w_out, ln_g3, ln_b3, i, 0,
                 src=layout, dst=mixer_layout)
        if i % 2 == 0:
            bmat, cmat, lam_r, lam_i = _s5_discretise(
                s5_a_re[j], s5_a_im[j], s5_log_dt[j], s5_b_re[j], s5_b_im[j],
                s5_c_re[j], s5_c_im[j])
            h = h.reshape(SEQ // 2, 2, BATCH, D_MODEL)
            h = _s5(h, s5_w_in[j].astype(_BF16), bmat, cmat, lam_r, lam_i,
                    _row(s5_d[j]), s5_w_glu[j].astype(_BF16),
                    _row(s5_b_glu[j]), s5_w_out[j].astype(_BF16),
                    _row(ln_g[i, 1]), _row(ln_b[i, 1]))
            h = h.reshape(_TALL)
        else:
            bias = jnp.broadcast_to(g_b_s[j][:, :, None],
                                    (GMLP_HEADS, GMLP_BLOCK, LANES))
            h = _gmlp(h, g_w_in[j].astype(_BF16), _row(g_b_in[j]),
                      _row(g_ln_g[j]), _row(g_ln_b[j]), g_w_s[j], bias,
                      g_w_out[j].astype(_BF16),
                      _row(ln_g[i, 1]), _row(ln_b[i, 1]))
        layout = "batch" if i == DEPTH - 1 else "wide"
        h = _ffn(h, w_in, w_out, ln_g3, ln_b3, i, 1,
                 src=mixer_layout, dst=layout)
    return h
```
